```python
import jax, jax.numpy as jnp
from jax import lax
import numpy as np

D_MODEL = 1024
BATCH = 4
SEQ = 8192
DEPTH = 2

CHUNK = 64
MIX = D_MODEL
HEAD_DIM = 64
A_WIDTH = 3 * MIX // 8
A_HEADS = A_WIDTH // HEAD_DIM
GMLP_BLOCK = 128
B_WIDTH = MIX // 4
B_GROUPS = 4
B_GROUP_DIM = B_WIDTH // B_GROUPS
POOL_WINDOWS = (2, 4, 8, 16)
C_WIDTH = MIX - A_WIDTH - B_WIDTH
C_KERNEL = 31
PROJ_WIDTH = 2 * A_WIDTH + B_WIDTH + 2 * C_WIDTH
N_MEM = 256
X_HEADS = 4
X_HEAD_DIM = D_MODEL // X_HEADS
D_FF = ((int(8 * D_MODEL / 3) + 127) // 128) * 128
FFN_KERNEL = 3
RMS_EPS = 1e-6
LN_EPS = 1e-5

kernel_name = "hybrid_chunk_causal_gmlp_pool_conformer_encoder"


def rms_norm(x, g):
    xf = x.astype(jnp.float32)
    y = xf * lax.rsqrt(jnp.mean(xf * xf, axis=-1, keepdims=True) + RMS_EPS)
    return (y * g.astype(jnp.float32)).astype(x.dtype)


def layer_norm(x, g, b):
    xf = x.astype(jnp.float32)
    mu = jnp.mean(xf, axis=-1, keepdims=True)
    xc = xf - mu
    var = jnp.mean(xc * xc, axis=-1, keepdims=True)
    y = xc * lax.rsqrt(var + LN_EPS)
    return (y * g.astype(jnp.float32) + b.astype(jnp.float32)).astype(x.dtype)


def causal_dwconv(x, w, b):
    k = w.shape[0]
    xp = jnp.pad(x, ((0, 0), (k - 1, 0), (0, 0)))
    y = lax.conv_general_dilated(
        xp, w[:, None, :], window_strides=(1,), padding='VALID',
        dimension_numbers=('NWC', 'WIO', 'NWC'), feature_group_count=x.shape[-1])
    return y + b


def gmlp_spatial_gate(z, ln_g, ln_b, w_s, b_s):
    z = jax.nn.gelu(z, approximate=False)
    u, v = jnp.split(z, 2, axis=-1)
    v = layer_norm(v, ln_g, ln_b)
    bsz, s, _ = v.shape
    nb = s // GMLP_BLOCK
    v = v.reshape(bsz, nb, GMLP_BLOCK, A_HEADS, HEAD_DIM)
    pos = jnp.arange(GMLP_BLOCK)
    mask = (pos[None, :] // CHUNK) <= (pos[:, None] // CHUNK)
    w = jnp.where(mask[None], w_s, 0.0)
    mixed = jnp.einsum('hij,bnjhc->bnihc', w, v) + b_s.T[:, :, None]
    return u * mixed.reshape(bsz, s, A_WIDTH)


def multiscale_pool(p, w_pool, b_pool, scale):
    bsz, s, _ = p.shape
    pg = p.reshape(bsz, s, B_GROUPS, B_GROUP_DIM)
    pf = pg.astype(jnp.float32)
    cs = jnp.pad(jnp.cumsum(pf, axis=1), ((0, 0), (1, 0), (0, 0), (0, 0)))
    t = jnp.arange(s)
    outs = []
    for g, win in enumerate(POOL_WINDOWS):
        lo = jnp.maximum(t + 1 - win, 0)
        cnt = (t + 1 - lo).astype(jnp.float32)
        outs.append((cs[:, 1:, g] - cs[:, lo, g]) / cnt[None, :, None])
    pooled = jnp.stack(outs, axis=2)
    y = (pooled - pf).astype(p.dtype)
    y = jnp.einsum('bsgc,gcd->bsgd', y, w_pool) + b_pool
    return y.reshape(bsz, s, B_WIDTH) * scale


def conformer_conv(z, conv_w, conv_b, ln_g, ln_b):
    a, g = jnp.split(z, 2, axis=-1)
    h = a * jax.nn.sigmoid(g)
    h = causal_dwconv(h, conv_w, conv_b)
    h = layer_norm(h, ln_g, ln_b)
    return jax.nn.silu(h)


def cross_attention(h, mem_n, wq, wk, wv, wo):
    bsz, s, _ = h.shape
    m = mem_n.shape[1]
    q = (h @ wq).reshape(bsz, s, X_HEADS, X_HEAD_DIM)
    k = (mem_n @ wk).reshape(bsz, m, X_HEADS, X_HEAD_DIM)
    v = (mem_n @ wv).reshape(bsz, m, X_HEADS, X_HEAD_DIM)
    sc = jnp.einsum('bshd,bmhd->bhsm', q, k).astype(jnp.float32) * (X_HEAD_DIM ** -0.5)
    pr = jax.nn.softmax(sc, axis=-1).astype(v.dtype)
    o = jnp.einsum('bhsm,bmhd->bshd', pr, v).reshape(bsz, s, D_MODEL)
    return o @ wo


def conv_ffn(h, w_up, conv_w, conv_b, w_down):
    u = h @ w_up
    u = causal_dwconv(u, conv_w, conv_b)
    g, v = jnp.split(u, 2, axis=-1)
    return (jax.nn.silu(g) * v) @ w_down


def setup_inputs(seed: int = 0) -> dict:
    key = jax.random.key(seed)
    ks = jax.random.split(key, 32)
    L = DEPTH
    f32 = jnp.float32

    def nrm(k, shape, scale):
        return jax.random.normal(k, shape, f32) * scale

    def gain(k, shape):
        return 1.0 + 0.02 * jax.random.normal(k, shape, f32)

    return {
        "x": jax.random.normal(ks[0], (BATCH, SEQ, D_MODEL), f32),
        "mem": jax.random.normal(ks[1], (BATCH, N_MEM, D_MODEL), f32),
        "norm_mix": gain(ks[2], (L, D_MODEL)),
        "w_in": nrm(ks[3], (L, D_MODEL, PROJ_WIDTH), D_MODEL ** -0.5),
        "gmlp_ln_g": gain(ks[4], (L, A_WIDTH)),
        "gmlp_ln_b": nrm(ks[5], (L, A_WIDTH), 0.02),
        "gmlp_ws": nrm(ks[6], (L, A_HEADS, GMLP_BLOCK, GMLP_BLOCK), GMLP_BLOCK ** -0.5),
        "gmlp_bs": gain(ks[7], (L, A_HEADS, GMLP_BLOCK)),
        "pool_w": nrm(ks[8], (L, B_GROUPS, B_GROUP_DIM, B_GROUP_DIM), B_GROUP_DIM ** -0.5),
        "pool_b": nrm(ks[9], (L, B_GROUPS, B_GROUP_DIM), 0.02),
        "pool_scale": gain(ks[10], (L, B_WIDTH)),
        "conv_w": nrm(ks[11], (L, C_KERNEL, C_WIDTH), C_KERNEL ** -0.5),
        "conv_b": nrm(ks[12], (L, C_WIDTH), 0.02),
        "conv_ln_g": gain(ks[13], (L, C_WIDTH)),
        "conv_ln_b": nrm(ks[14], (L, C_WIDTH), 0.02),
        "w_out": nrm(ks[15], (L, MIX, D_MODEL), MIX ** -0.5),
        "norm_x": gain(ks[16], (L, D_MODEL)),
        "norm_mem": gain(ks[17], (L, D_MODEL)),
        "wq": nrm(ks[18], (L, D_MODEL, D_MODEL), D_MODEL ** -0.5),
        "wk": nrm(ks[19], (L, D_MODEL, D_MODEL), D_MODEL ** -0.5),
        "wv": nrm(ks[20], (L, D_MODEL, D_MODEL), D_MODEL ** -0.5),
        "wo": nrm(ks[21], (L, D_MODEL, D_MODEL), D_MODEL ** -0.5),
        "norm_ffn": gain(ks[22], (L, D_MODEL)),
        "w_up": nrm(ks[23], (L, D_MODEL, 2 * D_FF), D_MODEL ** -0.5),
        "ffn_conv_w": nrm(ks[24], (L, FFN_KERNEL, 2 * D_FF), FFN_KERNEL ** -0.5),
        "ffn_conv_b": nrm(ks[25], (L, 2 * D_FF), 0.02),
        "w_down": nrm(ks[26], (L, D_FF, D_MODEL), D_FF ** -0.5),
        "norm_final": gain(ks[27], (D_MODEL,)),
    }


def reference(x, mem, norm_mix, w_in, gmlp_ln_g, gmlp_ln_b, gmlp_ws, gmlp_bs,
              pool_w, pool_b, pool_scale, conv_w, conv_b, conv_ln_g, conv_ln_b,
              w_out, norm_x, norm_mem, wq, wk, wv, wo, norm_ffn, w_up,
              ffn_conv_w, ffn_conv_b, w_down, norm_final):
    a_end = 2 * A_WIDTH
    b_end = a_end + B_WIDTH
    for l in range(DEPTH):
        h = rms_norm(x, norm_mix[l])
        z = h @ w_in[l]
        y_a = gmlp_spatial_gate(z[..., :a_end], gmlp_ln_g[l], gmlp_ln_b[l], gmlp_ws[l], gmlp_bs[l])
        y_b = multiscale_pool(z[..., a_end:b_end], pool_w[l], pool_b[l], pool_scale[l])
        y_c = conformer_conv(z[..., b_end:], conv_w[l], conv_b[l], conv_ln_g[l], conv_ln_b[l])
        y = jnp.concatenate([y_a, y_b, y_c], axis=-1)
        x = x + y @ w_out[l]
        h = rms_norm(x, norm_x[l])
        mem_n = rms_norm(mem, norm_mem[l])
        x = x + cross_attention(h, mem_n, wq[l], wk[l], wv[l], wo[l])
        h = rms_norm(x, norm_ffn[l])
        x = x + conv_ffn(h, w_up[l], ffn_conv_w[l], ffn_conv_b[l], w_down[l])
    return rms_norm(x, norm_final)
```

```python
import functools

import numpy as np
import jax
import jax.numpy as jnp
from jax import lax
from jax.experimental import pallas as pl
from jax.experimental.pallas import tpu as pltpu

D_MODEL = 1024
CHUNK = 64
HEAD_DIM = 64
A_WIDTH = 384
A_HEADS = 6
GMLP_BLOCK = 128
B_WIDTH = 256
B_GROUPS = 4
B_GROUP_DIM = 64
POOL_WINDOWS = (2, 4, 8, 16)
C_WIDTH = 384
C_KERNEL = 31
PROJ_WIDTH = 2 * A_WIDTH + B_WIDTH + 2 * C_WIDTH
X_HEADS = 4
X_HEAD_DIM = 256
D_FF = 2816
FFN_KERNEL = 3
RMS_EPS = 1e-6
LN_EPS = 1e-5

SUBLANES = 8
LANES = 128
VMEM_LIMIT_BYTES = 56 * 1024 * 1024

F32 = jnp.float32
BF16 = jnp.bfloat16

A_END = 2 * A_WIDTH
B_END = A_END + B_WIDTH
POOL_HALO = 16
CONV_HALO = 32
FFN_HALO = SUBLANES


def _rms_norm(x, g):
    return x * lax.rsqrt(jnp.mean(x * x, axis=-1, keepdims=True) + RMS_EPS) * g


def _layer_norm(x, g, b):
    mu = jnp.mean(x, axis=-1, keepdims=True)
    xc = x - mu
    var = jnp.mean(xc * xc, axis=-1, keepdims=True)
    return xc * lax.rsqrt(var + LN_EPS) * g + b


def _dot(a, b):
    return jnp.dot(a, b, preferred_element_type=F32)


def _mixer_kernel(x_ref, nrm_ref, win_ref, lng_ref, lnb_ref, ws_ref, bsf_ref, pw_ref, pb_ref,
                  psc_ref, cw_ref, cb_ref, clg_ref, clb_ref, wout_ref, o_ref,
                  hb_ref, z_ref, ppad_ref, hpad_ref, y_ref, *, ts):
    t = pl.program_id(1)
    rb = GMLP_BLOCK

    @pl.when(t == 0)
    def _():
        ppad_ref[0:POOL_HALO, :] = jnp.zeros((POOL_HALO, B_WIDTH), F32)
        hpad_ref[0:CONV_HALO, :] = jnp.zeros((CONV_HALO, C_WIDTH), F32)

    for r0 in range(0, ts, rb):
        hb_ref[r0:r0 + rb, :] = _rms_norm(x_ref[r0:r0 + rb, :], nrm_ref[...]).astype(BF16)
    z_ref[...] = _dot(hb_ref[...], win_ref[...])

    row = lax.broadcasted_iota(jnp.int32, (GMLP_BLOCK, 2 * GMLP_BLOCK), 0)
    col = lax.broadcasted_iota(jnp.int32, (GMLP_BLOCK, 2 * GMLP_BLOCK), 1)
    keep = ((col % GMLP_BLOCK) // CHUNK) <= (row // CHUNK)
    lane = lax.broadcasted_iota(jnp.int32, (rb, LANES), 1)
    low_half = lane < HEAD_DIM

    lane_b = lax.broadcasted_iota(jnp.int32, (1, B_WIDTH), 1)
    win = jnp.where(lane_b < 64, 2, jnp.where(lane_b < 128, 4, jnp.where(lane_b < 192, 8, 16)))

    for r0 in range(0, ts, rb):
        za = z_ref[r0:r0 + rb, 0:A_END]
        za = 0.5 * za * (1.0 + lax.erf(za * np.float32(np.sqrt(0.5))))
        u = za[:, :A_WIDTH]
        vn = _layer_norm(za[:, A_WIDTH:], lng_ref[...], lnb_ref[...])
        for p in range(A_HEADS // 2):
            vp = vn[:, p * LANES:(p + 1) * LANES]
            rhs = jnp.concatenate([jnp.where(low_half, vp, 0.0).astype(BF16),
                                   jnp.where(low_half, 0.0, vp).astype(BF16)], axis=0)
            wm = jnp.where(keep, ws_ref[p], 0.0).astype(BF16)
            mixed = _dot(wm, rhs) + bsf_ref[:, p * LANES:(p + 1) * LANES]
            y_ref[r0:r0 + rb, p * LANES:(p + 1) * LANES] = (
                u[:, p * LANES:(p + 1) * LANES] * mixed).astype(BF16)

        pcur = z_ref[r0:r0 + rb, A_END:B_END]
        ppad_ref[POOL_HALO + r0:POOL_HALO + r0 + rb, :] = pcur

        def shifted(k, c0):
            s = POOL_HALO + r0 - k
            return ppad_ref[s:s + rb, c0:c0 + LANES]

        s2a = pcur[:, :LANES] + shifted(1, 0)
        s4a = s2a + shifted(2, 0) + shifted(3, 0)
        s8b = pcur[:, LANES:]
        for k in range(1, 8):
            s8b = s8b + shifted(k, LANES)
        s16b = s8b
        for k in range(8, 16):
            s16b = s16b + shifted(k, LANES)
        sums = jnp.concatenate([jnp.where(low_half, s2a, s4a), jnp.where(low_half, s8b, s16b)], axis=1)
        tpos = t * ts + r0 + lax.broadcasted_iota(jnp.int32, (rb, 1), 0)
        cnt = jnp.minimum(tpos + 1, win).astype(F32)
        yb = (sums / cnt - pcur).astype(BF16)
        yb = (_dot(yb, pw_ref[...]) + pb_ref[...]) * psc_ref[...]
        y_ref[r0:r0 + rb, A_WIDTH:A_WIDTH + B_WIDTH] = yb.astype(BF16)

        a = z_ref[r0:r0 + rb, B_END:B_END + C_WIDTH]
        g = z_ref[r0:r0 + rb, B_END + C_WIDTH:PROJ_WIDTH]
        hpad_ref[CONV_HALO + r0:CONV_HALO + r0 + rb, :] = a * jax.nn.sigmoid(g)
        cr = 64
        for q0 in range(r0, r0 + rb, cr):
            acc = jnp.broadcast_to(cb_ref[...], (cr, C_WIDTH))
            for k in range(C_KERNEL):
                s = CONV_HALO + q0 - (C_KERNEL - 1) + k
                acc = acc + cw_ref[k:k + 1, :] * hpad_ref[s:s + cr, :]
            hc = _layer_norm(acc, clg_ref[...], clb_ref[...])
            y_ref[q0:q0 + cr, A_WIDTH + B_WIDTH:] = (hc * jax.nn.sigmoid(hc)).astype(BF16)

    ppad_ref[0:POOL_HALO, :] = ppad_ref[ts:ts + POOL_HALO, :]
    hpad_ref[0:CONV_HALO, :] = hpad_ref[ts:ts + CONV_HALO, :]

    o_ref[...] = x_ref[...] + _dot(y_ref[...], wout_ref[...])


def _const_spec(shape):
    nd = len(shape)
    return pl.BlockSpec(shape, lambda b, t: (0,) * nd, pipeline_mode=pl.Buffered(1))


def _tile_spec(ts):
    return pl.BlockSpec((None, ts, D_MODEL), lambda b, t: (b, t, 0))


def _mixer(x, nrm, win, lng, lnb, ws, bsf, pw, pb, psc, cw, cb, clg, clb, wout, *, ts):
    bsz, seq, _ = x.shape
    consts = (nrm, win, lng, lnb, ws, bsf, pw, pb, psc, cw, cb, clg, clb, wout)
    return pl.pallas_call(
        functools.partial(_mixer_kernel, ts=ts),
        out_shape=jax.ShapeDtypeStruct(x.shape, F32),
        grid=(bsz, seq // ts),
        in_specs=[_tile_spec(ts)] + [_const_spec(c.shape) for c in consts],
        out_specs=_tile_spec(ts),
        scratch_shapes=[
            pltpu.VMEM((ts, D_MODEL), BF16),
            pltpu.VMEM((ts, PROJ_WIDTH), F32),
            pltpu.VMEM((POOL_HALO + ts, B_WIDTH), F32),
            pltpu.VMEM((CONV_HALO + ts, C_WIDTH), F32),
            pltpu.VMEM((ts, D_MODEL), BF16),
        ],
        compiler_params=pltpu.CompilerParams(
            dimension_semantics=("arbitrary", "arbitrary"), vmem_limit_bytes=VMEM_LIMIT_BYTES),
        name="mixer",
    )(x, *consts)


def _memkv_kernel(mem_ref, nrm_ref, wk_ref, wv_ref, kt_ref, v_ref):
    mn = _rms_norm(mem_ref[...], nrm_ref[...]).astype(BF16)
    kt_ref[...] = _dot(mn, wk_ref[...]).T.astype(BF16)
    v_ref[...] = _dot(mn, wv_ref[...]).astype(BF16)


def _memkv(mem, nrm, wk, wv):
    bsz, n_mem, _ = mem.shape
    return pl.pallas_call(
        _memkv_kernel,
        out_shape=(jax.ShapeDtypeStruct((bsz, D_MODEL, n_mem), BF16),
                   jax.ShapeDtypeStruct((bsz, n_mem, D_MODEL), BF16)),
        grid=(bsz,),
        in_specs=[pl.BlockSpec((None, n_mem, D_MODEL), lambda b: (b, 0, 0)),
                  pl.BlockSpec(nrm.shape, lambda b: (0, 0)),
                  pl.BlockSpec(wk.shape, lambda b: (0, 0)),
                  pl.BlockSpec(wv.shape, lambda b: (0, 0))],
        out_specs=(pl.BlockSpec((None, D_MODEL, n_mem), lambda b: (b, 0, 0)),
                   pl.BlockSpec((None, n_mem, D_MODEL), lambda b: (b, 0, 0))),
        compiler_params=pltpu.CompilerParams(
            dimension_semantics=("arbitrary",), vmem_limit_bytes=VMEM_LIMIT_BYTES),
        name="memkv",
    )(mem, nrm, wk, wv)


def _xattn_kernel(x_ref, nrm_ref, wq_ref, kt_ref, v_ref, wo_ref, o_ref, q_ref, a_ref):
    hb = _rms_norm(x_ref[...], nrm_ref[...]).astype(BF16)
    q_ref[...] = (_dot(hb, wq_ref[...]) * (X_HEAD_DIM ** -0.5)).astype(BF16)
    for h in range(X_HEADS):
        c0 = h * X_HEAD_DIM
        s = _dot(q_ref[:, c0:c0 + X_HEAD_DIM], kt_ref[c0:c0 + X_HEAD_DIM, :])
        e = jnp.exp(s - jnp.max(s, axis=-1, keepdims=True))
        l = jnp.sum(e, axis=-1, keepdims=True)
        o = _dot(e.astype(BF16), v_ref[:, c0:c0 + X_HEAD_DIM]) / l
        a_ref[:, c0:c0 + X_HEAD_DIM] = o.astype(BF16)
    o_ref[...] = x_ref[...] + _dot(a_ref[...], wo_ref[...])


def _xattn(x, nrm, wq, kt, v, wo, *, ts):
    bsz, seq, _ = x.shape
    n_mem = v.shape[1]
    return pl.pallas_call(
        _xattn_kernel,
        out_shape=jax.ShapeDtypeStruct(x.shape, F32),
        grid=(bsz, seq // ts),
        in_specs=[_tile_spec(ts), _const_spec(nrm.shape), _const_spec(wq.shape),
                  pl.BlockSpec((None, D_MODEL, n_mem), lambda b, t: (b, 0, 0)),
                  pl.BlockSpec((None, n_mem, D_MODEL), lambda b, t: (b, 0, 0)),
                  _const_spec(wo.shape)],
        out_specs=_tile_spec(ts),
        scratch_shapes=[pltpu.VMEM((ts, D_MODEL), BF16), pltpu.VMEM((ts, D_MODEL), BF16)],
        compiler_params=pltpu.CompilerParams(
            dimension_semantics=("arbitrary", "arbitrary"), vmem_limit_bytes=VMEM_LIMIT_BYTES),
        name="xattn",
    )(x, nrm, wq, kt, v, wo)


FFN_COL_CHUNK = D_FF // 2
FFN_ROW_BLOCK = 64
FFN_COL_BLOCK = 128


def _convffn_kernel(x_ref, nrm_ref, wup_ref, cw_ref, cb_ref, wdown_ref, fin_ref, o_ref,
                    hb_ref, upad_ref, carry_ref, act_ref, acc_ref, *, ts, final_norm):
    t = pl.program_id(1)
    cwid = FFN_COL_CHUNK

    @pl.when(t == 0)
    def _():
        carry_ref[...] = jnp.zeros(carry_ref.shape, F32)

    hb_ref[...] = _rms_norm(x_ref[...], nrm_ref[...]).astype(BF16)
    acc_ref[...] = x_ref[...]

    for j in range(D_FF // cwid):
        cols = (j * cwid, D_FF + j * cwid)
        for part, c0 in enumerate(cols):
            upad_ref[part, 0:FFN_HALO, :] = carry_ref[:, c0:c0 + cwid]
            upad_ref[part, FFN_HALO:FFN_HALO + ts, :] = _dot(hb_ref[...], wup_ref[:, c0:c0 + cwid])
            carry_ref[:, c0:c0 + cwid] = upad_ref[part, ts:ts + FFN_HALO, :]

        def conv(part, r0, b0):
            c0 = cols[part] + b0
            out = cb_ref[:, c0:c0 + FFN_COL_BLOCK]
            for k in range(FFN_KERNEL):
                s = FFN_HALO + r0 - (FFN_KERNEL - 1) + k
                out = out + (cw_ref[k:k + 1, c0:c0 + FFN_COL_BLOCK]
                             * upad_ref[part, s:s + FFN_ROW_BLOCK, b0:b0 + FFN_COL_BLOCK])
            return out

        for r0 in range(0, ts, FFN_ROW_BLOCK):
            for b0 in range(0, cwid, FFN_COL_BLOCK):
                g = conv(0, r0, b0)
                v = conv(1, r0, b0)
                act_ref[r0:r0 + FFN_ROW_BLOCK, b0:b0 + FFN_COL_BLOCK] = (
                    g * jax.nn.sigmoid(g) * v).astype(BF16)
        acc_ref[...] += _dot(act_ref[...], wdown_ref[j * cwid:(j + 1) * cwid, :])

    if final_norm:
        o_ref[...] = _rms_norm(acc_ref[...], fin_ref[...])
    else:
        o_ref[...] = acc_ref[...]


def _convffn(x, nrm, wup, cw, cb, wdown, fin, *, ts, final_norm):
    bsz, seq, _ = x.shape
    consts = (nrm, wup, cw, cb, wdown, fin)
    return pl.pallas_call(
        functools.partial(_convffn_kernel, ts=ts, final_norm=final_norm),
        out_shape=jax.ShapeDtypeStruct(x.shape, F32),
        grid=(bsz, seq // ts),
        in_specs=[_tile_spec(ts)] + [_const_spec(c.shape) for c in consts],
        out_specs=_tile_spec(ts),
        scratch_shapes=[
            pltpu.VMEM((ts, D_MODEL), BF16),
            pltpu.VMEM((2, FFN_HALO + ts, FFN_COL_CHUNK), F32),
            pltpu.VMEM((FFN_HALO, 2 * D_FF), F32),
            pltpu.VMEM((ts, FFN_COL_CHUNK), BF16),
            pltpu.VMEM((ts, D_MODEL), F32),
        ],
        compiler_params=pltpu.CompilerParams(
            dimension_semantics=("arbitrary", "arbitrary"), vmem_limit_bytes=VMEM_LIMIT_BYTES),
        name="convffn",
    )(x, *consts)


MIXER_TILE = 256
XATTN_TILE = 256
FFN_TILE = 256


def _row(v):
    return v.reshape(1, -1)


def kernel(x, mem, norm_mix, w_in, gmlp_ln_g, gmlp_ln_b, gmlp_ws, gmlp_bs, pool_w, pool_b, pool_scale, conv_w, conv_b, conv_ln_g, conv_ln_b, w_out, norm_x, norm_mem, wq, wk, wv, wo, norm_ffn, w_up, ffn_conv_w, ffn_conv_b, w_down, norm_final):
    depth = w_in.shape[0]
    for l in range(depth):
        ws_pairs = gmlp_ws[l].reshape(A_HEADS // 2, 2, GMLP_BLOCK, GMLP_BLOCK)
        ws_pairs = ws_pairs.transpose(0, 2, 1, 3).reshape(A_HEADS // 2, GMLP_BLOCK, 2 * GMLP_BLOCK)
        bs_full = jnp.repeat(gmlp_bs[l].T, HEAD_DIM, axis=1)
        pw_bd = jax.scipy.linalg.block_diag(*[pool_w[l, g] for g in range(B_GROUPS)]).astype(BF16)

        x = _mixer(x, _row(norm_mix[l]), w_in[l].astype(BF16), _row(gmlp_ln_g[l]), _row(gmlp_ln_b[l]),
                   ws_pairs, bs_full, pw_bd, _row(pool_b[l]), _row(pool_scale[l]),
                   conv_w[l], _row(conv_b[l]), _row(conv_ln_g[l]), _row(conv_ln_b[l]),
                   w_out[l].astype(BF16), ts=MIXER_TILE)

        kt, v = _memkv(mem, _row(norm_mem[l]), wk[l].astype(BF16), wv[l].astype(BF16))
        x = _xattn(x, _row(norm_x[l]), wq[l].astype(BF16), kt, v, wo[l].astype(BF16), ts=XATTN_TILE)

        x = _convffn(x, _row(norm_ffn[l]), w_up[l].astype(BF16), ffn_conv_w[l], _row(ffn_conv_b[l]),
                     w_down[l].astype(BF16), _row(norm_final), ts=FFN_TILE,
                     final_norm=(l == depth - 1))
    return x
```
